```python
import math
import jax, jax.numpy as jnp
from jax import lax
import numpy as np

D_MODEL = 1024
BATCH = 8
SEQ = 2048
DEPTH = 1
DEC_BATCH = 128
DEC_SEQ = 1
PAST_LEN = 16384
PAGE_SIZE = 128

D_MIX = D_MODEL
C_CONV = D_MIX // 2
CONV_WIDTH = 31
DN_HEADS = 4
DN_DK = (D_MIX - C_CONV) // DN_HEADS
DN_DV = DN_DK
DN_QK = DN_HEADS * DN_DK
DN_V = DN_HEADS * DN_DV
QKV_COLS = 2 * DN_QK + DN_V
SHORT_CONV = 4
DN_CHUNK = 64
N_MEM = 256
MEM_HEADS = 4
MEM_HD = D_MODEL // MEM_HEADS
D_FF = -(-8 * D_MODEL // (3 * 256)) * 256

O_GLU_A = 0
O_GLU_B = C_CONV
O_QKV = 2 * C_CONV
O_Z = O_QKV + QKV_COLS
O_BETA = O_Z + DN_V
O_DECAY = O_BETA + DN_HEADS
IN_COLS = O_DECAY + DN_HEADS

kernel_name = 'hymba_conformer_gdn_memxattn_step'


def rms_norm(x, g, eps=1e-6):
    xf = x.astype(jnp.float32)
    y = xf * lax.rsqrt(jnp.mean(xf * xf, axis=-1, keepdims=True) + eps)
    return (y * g.astype(jnp.float32)).astype(x.dtype)


def layer_norm(x, g, b, eps=1e-5):
    xf = x.astype(jnp.float32)
    mu = jnp.mean(xf, axis=-1, keepdims=True)
    xc = xf - mu
    y = xc * lax.rsqrt(jnp.mean(xc * xc, axis=-1, keepdims=True) + eps)
    return (y * g.astype(jnp.float32) + b.astype(jnp.float32)).astype(x.dtype)


def l2_norm(x, eps=1e-6):
    return x * lax.rsqrt(jnp.sum(x * x, axis=-1, keepdims=True) + eps)


def causal_depthwise_conv(x, buf, w):
    xp = jnp.concatenate([buf.astype(x.dtype), x], axis=1)
    y = lax.conv_general_dilated(xp, w[:, None, :].astype(x.dtype), window_strides=(1,), padding='VALID',
                                 dimension_numbers=('NWC', 'WIO', 'NWC'), feature_group_count=x.shape[-1])
    return y, xp[:, xp.shape[1] - (w.shape[0] - 1):]


def gated_delta_chunked(q, k, v, g, beta, s0):
    bsz, seq = q.shape[0], q.shape[1]
    c = min(DN_CHUNK, seq)
    n = -(-seq // c)
    pad = n * c - seq
    if pad:
        padw = lambda t: jnp.pad(t, [(0, 0), (0, pad)] + [(0, 0)] * (t.ndim - 2))
        q, k, v, g, beta = padw(q), padw(k), padw(v), padw(g), padw(beta)

    def chunks(t):
        t = t.reshape((bsz, n, c) + t.shape[2:])
        return jnp.swapaxes(jnp.swapaxes(t, 0, 1), 2, 3)

    q, k, v, g, beta = chunks(q), chunks(k), chunks(v), chunks(g), chunks(beta)
    gc = jnp.cumsum(g, axis=-1)
    incl = jnp.tril(jnp.ones((c, c), dtype=bool))
    strict = jnp.tril(jnp.ones((c, c), dtype=bool), k=-1)
    decay = jnp.exp(jnp.where(incl, gc[..., :, None] - gc[..., None, :], -jnp.inf))
    kb = k * beta[..., None]
    lower = jnp.where(strict, jnp.einsum('nbhid,nbhjd->nbhij', kb, k) * decay, 0.0)
    a_mat = lower + jnp.eye(c, dtype=lower.dtype)
    u = lax.linalg.triangular_solve(a_mat, v * beta[..., None], left_side=True, lower=True, unit_diagonal=True)
    w = lax.linalg.triangular_solve(a_mat, kb * jnp.exp(gc)[..., None], left_side=True, lower=True,
                                    unit_diagonal=True)
    qk = jnp.where(incl, jnp.einsum('nbhid,nbhjd->nbhij', q, k) * decay, 0.0)

    def step(s, xs):
        q_i, k_i, u_i, w_i, g_i, qk_i = xs
        v_new = u_i - jnp.einsum('bhcd,bhde->bhce', w_i, s)
        o_i = (jnp.einsum('bhcd,bhde->bhce', q_i * jnp.exp(g_i)[..., None], s)
               + jnp.einsum('bhij,bhje->bhie', qk_i, v_new))
        g_last = g_i[..., -1]
        k_dec = k_i * jnp.exp(g_last[..., None] - g_i)[..., None]
        s = s * jnp.exp(g_last)[..., None, None] + jnp.einsum('bhcd,bhce->bhde', k_dec, v_new)
        return s, o_i

    s_fin, o = lax.scan(step, s0, (q, k, u, w, gc, qk))
    o = jnp.swapaxes(jnp.swapaxes(o, 2, 3), 0, 1).reshape(bsz, n * c, o.shape[2], o.shape[-1])[:, :seq]
    return o, s_fin


def parallel_mixer(h, conv_buf, sc_buf, s0, w_in, conv_w, conv_b, conv_ln_g, conv_ln_b, sc_w, a_log, dt_bias,
                   dn_norm, w_out):
    bsz, seq, _ = h.shape
    p = h @ w_in
    u = p[..., O_GLU_A:O_GLU_B] * jax.nn.sigmoid(p[..., O_GLU_B:O_QKV])
    c, new_conv_buf = causal_depthwise_conv(u, conv_buf, conv_w)
    c = jax.nn.silu(layer_norm(c + conv_b, conv_ln_g, conv_ln_b))
    qkv, new_sc_buf = causal_depthwise_conv(p[..., O_QKV:O_Z], sc_buf, sc_w)
    qkv = jax.nn.silu(qkv).astype(jnp.float32)
    q = l2_norm(qkv[..., :DN_QK].reshape(bsz, seq, DN_HEADS, DN_DK)) * (DN_DK ** -0.5)
    k = l2_norm(qkv[..., DN_QK:2 * DN_QK].reshape(bsz, seq, DN_HEADS, DN_DK))
    v = qkv[..., 2 * DN_QK:].reshape(bsz, seq, DN_HEADS, DN_DV)
    beta = jax.nn.sigmoid(p[..., O_BETA:O_DECAY].astype(jnp.float32))
    g = -jnp.exp(a_log.astype(jnp.float32)) * jax.nn.softplus(
        p[..., O_DECAY:IN_COLS].astype(jnp.float32) + dt_bias.astype(jnp.float32))
    o, s_new = gated_delta_chunked(q, k, v, g, beta, s0.astype(jnp.float32))
    z = p[..., O_Z:O_BETA].reshape(bsz, seq, DN_HEADS, DN_DV).astype(jnp.float32)
    o = rms_norm(o, dn_norm) * jax.nn.silu(z)
    d = o.reshape(bsz, seq, DN_V).astype(h.dtype)
    y = jnp.concatenate([c, d], axis=-1) @ w_out
    return y, new_conv_buf, new_sc_buf, s_new.astype(h.dtype)


def memory_kv(mem, norm_mem_kv, w_mk, w_mv):
    bsz, n_mem, _ = mem.shape
    m = rms_norm(mem, norm_mem_kv)
    k = (m @ w_mk).reshape(bsz, n_mem, MEM_HEADS, MEM_HD)
    v = (m @ w_mv).reshape(bsz, n_mem, MEM_HEADS, MEM_HD)
    return k, v


def memory_attend(h, mem_k, mem_v, w_mq, w_mo):
    bsz, seq, _ = h.shape
    q = (h @ w_mq).reshape(bsz, seq, MEM_HEADS, MEM_HD)
    s = jnp.einsum('blhd,bmhd->bhlm', q.astype(jnp.float32), mem_k.astype(jnp.float32)) * (MEM_HD ** -0.5)
    pr = jax.nn.softmax(s, axis=-1)
    o = jnp.einsum('bhlm,bmhd->blhd', pr, mem_v.astype(jnp.float32)).astype(h.dtype)
    return o.reshape(bsz, seq, MEM_HEADS * MEM_HD) @ w_mo


def swiglu(h, w_gate, w_up, w_down):
    return (jax.nn.silu(h @ w_gate) * (h @ w_up)) @ w_down


def decoder_layer(x, conv_buf, sc_buf, s0, mem_k, mem_v, norm_mix, w_in, conv_w, conv_b, conv_ln_g, conv_ln_b,
                  sc_w, a_log, dt_bias, dn_norm, w_out, norm_mem_q, w_mq, w_mo, norm_ffn, w_gate, w_up, w_down):
    y, conv_buf, sc_buf, s = parallel_mixer(rms_norm(x, norm_mix), conv_buf, sc_buf, s0, w_in, conv_w, conv_b,
                                            conv_ln_g, conv_ln_b, sc_w, a_log, dt_bias, dn_norm, w_out)
    x = x + y
    x = x + memory_attend(rms_norm(x, norm_mem_q), mem_k, mem_v, w_mq, w_mo)
    x = x + swiglu(rms_norm(x, norm_ffn), w_gate, w_up, w_down)
    return x, conv_buf, sc_buf, s


def setup_inputs(seed: int = 0) -> dict:
    key = jax.random.key(seed)
    ks = jax.random.split(key, 32)

    def nrm(k, shape, scale):
        return jax.random.normal(k, shape, jnp.float32) * scale

    def gain(k, shape):
        return 1.0 + 0.05 * jax.random.normal(k, shape, jnp.float32)

    dt = jnp.exp(jax.random.uniform(ks[10], (DEPTH, DN_HEADS), jnp.float32, math.log(1e-3), math.log(1e-1)))
    return {
        'x_prompt': nrm(ks[0], (BATCH, SEQ, D_MODEL), 1.0),
        'x_sample': nrm(ks[1], (DEC_BATCH, DEC_SEQ, D_MODEL), 1.0),
        'mem_prompt': nrm(ks[2], (BATCH, N_MEM, D_MODEL), 1.0),
        'cache_conv': nrm(ks[3], (DEPTH, DEC_BATCH, CONV_WIDTH - 1, C_CONV), 0.5),
        'state_short_conv': nrm(ks[4], (DEPTH, DEC_BATCH, SHORT_CONV - 1, QKV_COLS), 1.0),
        'state_delta': nrm(ks[5], (DEPTH, DEC_BATCH, DN_HEADS, DN_DK, DN_DV), 0.1),
        'cache_mem_k': nrm(ks[6], (DEPTH, DEC_BATCH, N_MEM, MEM_HEADS, MEM_HD), 1.0),
        'cache_mem_v': nrm(ks[7], (DEPTH, DEC_BATCH, N_MEM, MEM_HEADS, MEM_HD), 1.0),
        'norm_mix': gain(ks[8], (DEPTH, D_MODEL)),
        'w_in': nrm(ks[9], (DEPTH, D_MODEL, IN_COLS), D_MODEL ** -0.5),
        'conv_w': nrm(ks[11], (DEPTH, CONV_WIDTH, C_CONV), CONV_WIDTH ** -0.5),
        'conv_b': nrm(ks[12], (DEPTH, C_CONV), 0.02),
        'conv_ln_g': gain(ks[13], (DEPTH, C_CONV)),
        'conv_ln_b': nrm(ks[14], (DEPTH, C_CONV), 0.02),
        'sc_w': nrm(ks[15], (DEPTH, SHORT_CONV, QKV_COLS), SHORT_CONV ** -0.5),
        'a_log': jnp.log(jax.random.uniform(ks[16], (DEPTH, DN_HEADS), jnp.float32, 1.0, 16.0)),
        'dt_bias': dt + jnp.log(-jnp.expm1(-dt)),
        'dn_norm': gain(ks[17], (DEPTH, DN_DV)),
        'w_out': nrm(ks[18], (DEPTH, D_MIX, D_MODEL), D_MIX ** -0.5),
        'norm_mem_q': gain(ks[19], (DEPTH, D_MODEL)),
        'norm_mem_kv': gain(ks[20], (DEPTH, D_MODEL)),
        'w_mq': nrm(ks[21], (DEPTH, D_MODEL, MEM_HEADS * MEM_HD), D_MODEL ** -0.5),
        'w_mk': nrm(ks[22], (DEPTH, D_MODEL, MEM_HEADS * MEM_HD), D_MODEL ** -0.5),
        'w_mv': nrm(ks[23], (DEPTH, D_MODEL, MEM_HEADS * MEM_HD), D_MODEL ** -0.5),
        'w_mo': nrm(ks[24], (DEPTH, MEM_HEADS * MEM_HD, D_MODEL), (MEM_HEADS * MEM_HD) ** -0.5),
        'norm_ffn': gain(ks[25], (DEPTH, D_MODEL)),
        'w_gate': nrm(ks[26], (DEPTH, D_MODEL, D_FF), D_MODEL ** -0.5),
        'w_up': nrm(ks[27], (DEPTH, D_MODEL, D_FF), D_MODEL ** -0.5),
        'w_down': nrm(ks[28], (DEPTH, D_FF, D_MODEL), D_FF ** -0.5),
        'norm_f': gain(ks[29], (D_MODEL,)),
    }


def reference(x_prompt, x_sample, mem_prompt, cache_conv, state_short_conv, state_delta, cache_mem_k, cache_mem_v,
              norm_mix, w_in, conv_w, conv_b, conv_ln_g, conv_ln_b, sc_w, a_log, dt_bias, dn_norm, w_out,
              norm_mem_q, norm_mem_kv, w_mq, w_mk, w_mv, w_mo, norm_ffn, w_gate, w_up, w_down, norm_f):
    bp = x_prompt.shape[0]
    dt_ = x_prompt.dtype
    xp, xs = x_prompt, x_sample
    conv_p, sc_p, dl_p, mk_p, mv_p = [], [], [], [], []
    conv_s, sc_s, dl_s = [], [], []
    for l in range(DEPTH):
        lw = (norm_mix[l], w_in[l], conv_w[l], conv_b[l], conv_ln_g[l], conv_ln_b[l], sc_w[l], a_log[l],
              dt_bias[l], dn_norm[l], w_out[l], norm_mem_q[l], w_mq[l], w_mo[l], norm_ffn[l], w_gate[l],
              w_up[l], w_down[l])
        mk, mv = memory_kv(mem_prompt, norm_mem_kv[l], w_mk[l], w_mv[l])
        xp, cb, sb, st = decoder_layer(
            xp, jnp.zeros((bp, CONV_WIDTH - 1, C_CONV), dt_), jnp.zeros((bp, SHORT_CONV - 1, QKV_COLS), dt_),
            jnp.zeros((bp, DN_HEADS, DN_DK, DN_DV), dt_), mk, mv, *lw)
        conv_p.append(cb); sc_p.append(sb); dl_p.append(st); mk_p.append(mk); mv_p.append(mv)
        xs, cb, sb, st = decoder_layer(xs, cache_conv[l], state_short_conv[l], state_delta[l], cache_mem_k[l],
                                       cache_mem_v[l], *lw)
        conv_s.append(cb); sc_s.append(sb); dl_s.append(st)
    y_prompt = rms_norm(xp, norm_f)
    y_sample = rms_norm(xs, norm_f)
    return (y_prompt, y_sample, jnp.stack(conv_p), jnp.stack(sc_p), jnp.stack(dl_p), jnp.stack(mk_p),
            jnp.stack(mv_p), jnp.stack(conv_s), jnp.stack(sc_s), jnp.stack(dl_s))
```

```python
import functools

import jax
import jax.numpy as jnp
from jax import lax
from jax.experimental import pallas as pl
from jax.experimental.pallas import tpu as pltpu

F32 = jnp.float32
BF16 = jnp.bfloat16

D_MODEL = 1024
C_CONV = 512
CONV_WIDTH = 31
DN_HEADS = 4
DN_DK = 128
DN_V = 512
QKV_COLS = 1536
SHORT_CONV = 4
N_MEM = 256
MEM_HEADS = 4
MEM_HD = 256
D_FF = 2816
MAIN_COLS = 3072
O_QKV = 1024
O_Z = 2560

LANES = 128
SUBLANES = 8
VMEM_LIMIT = 52 * 1024 * 1024

TM_A = 256
U_HALO = 32
Q_HALO = 8
CONV_ROWS = 64
GDN_BLOCK = 256
GDN_CHUNK = 128
TM_O = 512
TQ_ATT = 512
TM_F = 512
FF_BLK = 256
SB_GDN = 16
SB_ATT = 16


def _dot(a, b):
    return jnp.dot(a.astype(BF16), b.astype(BF16), preferred_element_type=F32)


def _dot_nt(a, b):
    return lax.dot_general(a.astype(BF16), b.astype(BF16), (((1,), (1,)), ((), ())),
                           preferred_element_type=F32)


def _sigmoid(x):
    return 1.0 / (1.0 + jnp.exp(-x))


def _silu(x):
    return x * _sigmoid(x)


def _softplus(x):
    return jnp.maximum(x, 0.0) + jnp.log1p(jnp.exp(-jnp.abs(x)))


def _rms(x, g, eps=1e-6):
    return x * lax.rsqrt(jnp.mean(x * x, axis=-1, keepdims=True) + eps) * g


def _layer_norm(x, g, b, eps=1e-5):
    mu = jnp.mean(x, axis=-1, keepdims=True)
    xc = x - mu
    return xc * lax.rsqrt(jnp.mean(xc * xc, axis=-1, keepdims=True) + eps) * g + b


def _qkv_heads(qkv):
    qs, ks = [], []
    for h in range(DN_HEADS):
        q = qkv[:, h * DN_DK:(h + 1) * DN_DK]
        k = qkv[:, DN_V + h * DN_DK:DN_V + (h + 1) * DN_DK]
        qs.append(q * lax.rsqrt(jnp.sum(q * q, axis=-1, keepdims=True) + 1e-6) * (DN_DK ** -0.5))
        ks.append(k * lax.rsqrt(jnp.sum(k * k, axis=-1, keepdims=True) + 1e-6))
    return jnp.concatenate(qs, axis=1), jnp.concatenate(ks, axis=1), qkv[:, 2 * DN_V:]


def _beta_g(pbd, alog, dtb):
    lane = lax.broadcasted_iota(jnp.int32, pbd.shape, 1)
    g = -jnp.exp(alog) * _softplus(pbd + dtb)
    return jnp.where(lane < DN_HEADS, _sigmoid(pbd), g)


def _const_spec(shape):
    zeros = (0,) * len(shape)
    return pl.BlockSpec(shape, lambda *_: zeros, pipeline_mode=pl.Buffered(1))


def _full_spec(shape):
    zeros = (0,) * len(shape)
    return pl.BlockSpec(shape, lambda *_: zeros)


def _inproj_prompt_kernel(x_ref, nmix_ref, wmain_ref, wbd_ref, convw_ref, convb_ref, lng_ref, lnb_ref,
                          scw_ref, alog_ref, dtb_ref,
                          c_ref, q_ref, k_ref, v_ref, z_ref, bg_ref, nconv_ref, nsc_ref,
                          ubuf, qbuf):
    t = pl.program_id(1)
    tm = x_ref.shape[0]

    @pl.when(t == 0)
    def _():
        ubuf[0:U_HALO, :] = jnp.zeros((U_HALO, C_CONV), F32)
        qbuf[0:Q_HALO, :] = jnp.zeros((Q_HALO, QKV_COLS), F32)

    h = _rms(x_ref[...], nmix_ref[...]).astype(BF16)

    pg = jnp.dot(h, wmain_ref[:, 0:O_QKV], preferred_element_type=F32)
    ubuf[U_HALO:U_HALO + tm, :] = pg[:, :C_CONV] * _sigmoid(pg[:, C_CONV:])
    qbuf[Q_HALO:Q_HALO + tm, :] = jnp.dot(h, wmain_ref[:, O_QKV:O_Z], preferred_element_type=F32)
    z_ref[...] = jnp.dot(h, wmain_ref[:, O_Z:MAIN_COLS], preferred_element_type=F32)
    pbd = jnp.dot(h, wbd_ref[...], preferred_element_type=F32)
    bg_ref[...] = _beta_g(pbd, alog_ref[...], dtb_ref[...])

    off_u = U_HALO - (CONV_WIDTH - 1)
    for rb in range(tm // CONV_ROWS):
        r0 = rb * CONV_ROWS
        acc = jnp.zeros((CONV_ROWS, C_CONV), F32)
        for j in range(CONV_WIDTH):
            acc = acc + convw_ref[j:j + 1, :] * ubuf[r0 + off_u + j:r0 + off_u + j + CONV_ROWS, :]
        cn = _layer_norm(acc + convb_ref[...], lng_ref[...], lnb_ref[...])
        c_ref[r0:r0 + CONV_ROWS, :] = _silu(cn).astype(c_ref.dtype)

    off_q = Q_HALO - (SHORT_CONV - 1)
    for rb in range(tm // CONV_ROWS):
        r0 = rb * CONV_ROWS
        acc = jnp.zeros((CONV_ROWS, QKV_COLS), F32)
        for j in range(SHORT_CONV):
            acc = acc + scw_ref[j:j + 1, :] * qbuf[r0 + off_q + j:r0 + off_q + j + CONV_ROWS, :]
        q, k, v = _qkv_heads(_silu(acc))
        q_ref[r0:r0 + CONV_ROWS, :] = q
        k_ref[r0:r0 + CONV_ROWS, :] = k
        v_ref[r0:r0 + CONV_ROWS, :] = v

    @pl.when(t == pl.num_programs(1) - 1)
    def _():
        nconv_ref[0] = ubuf[tm + off_u:tm + U_HALO, :]
        nsc_ref[0] = qbuf[tm + off_q:tm + Q_HALO, :]

    ubuf[0:U_HALO, :] = ubuf[tm:tm + U_HALO, :]
    qbuf[0:Q_HALO, :] = qbuf[tm:tm + Q_HALO, :]


def _inproj_prompt(x2d, batch, seq, nmix, wmain, wbd, convw, convb, lng, lnb, scw, alog, dtb):
    tm = TM_A
    nt = seq // tm
    rows = batch * seq
    tok = lambda cols: pl.BlockSpec((tm, cols), lambda b, t: (b * nt + t, 0))
    return pl.pallas_call(
        _inproj_prompt_kernel,
        grid=(batch, nt),
        in_specs=[tok(D_MODEL), _const_spec((1, D_MODEL)), _const_spec((D_MODEL, MAIN_COLS)),
                  _const_spec((D_MODEL, LANES)), _const_spec((CONV_WIDTH, C_CONV)),
                  _const_spec((1, C_CONV)), _const_spec((1, C_CONV)), _const_spec((1, C_CONV)),
                  _const_spec((SHORT_CONV, QKV_COLS)), _const_spec((1, LANES)), _const_spec((1, LANES))],
        out_specs=[tok(C_CONV), tok(DN_V), tok(DN_V), tok(DN_V), tok(DN_V), tok(LANES),
                   pl.BlockSpec((1, CONV_WIDTH - 1, C_CONV), lambda b, t: (b, 0, 0)),
                   pl.BlockSpec((1, SHORT_CONV - 1, QKV_COLS), lambda b, t: (b, 0, 0))],
        out_shape=[jax.ShapeDtypeStruct((rows, C_CONV), BF16),
                   jax.ShapeDtypeStruct((rows, DN_V), F32),
                   jax.ShapeDtypeStruct((rows, DN_V), F32),
                   jax.ShapeDtypeStruct((rows, DN_V), F32),
                   jax.ShapeDtypeStruct((rows, DN_V), F32),
                   jax.ShapeDtypeStruct((rows, LANES), F32),
                   jax.ShapeDtypeStruct((batch, CONV_WIDTH - 1, C_CONV), F32),
                   jax.ShapeDtypeStruct((batch, SHORT_CONV - 1, QKV_COLS), F32)],
        scratch_shapes=[pltpu.VMEM((U_HALO + tm, C_CONV), F32), pltpu.VMEM((Q_HALO + tm, QKV_COLS), F32)],
        compiler_params=pltpu.CompilerParams(dimension_semantics=("arbitrary", "arbitrary"),
                                             vmem_limit_bytes=VMEM_LIMIT),
        name="inproj_prompt",
    )(x2d, nmix, wmain, wbd, convw, convb, lng, lnb, scw, alog, dtb)


def _split3(x):
    x1 = x.astype(BF16)
    r1 = x - x1.astype(F32)
    x2 = r1.astype(BF16)
    x3 = (r1 - x2.astype(F32)).astype(BF16)
    return x1, x2, x3


def _gdn_prompt_kernel(q_ref, k_ref, v_ref, z_ref, bg_ref, dn_ref, d_ref, snew_ref, s_ref):
    t = pl.program_id(1)
    n = q_ref.shape[0]
    c = GDN_CHUNK

    @pl.when(t == 0)
    def _():
        s_ref[...] = jnp.zeros(s_ref.shape, F32)

    row = lax.broadcasted_iota(jnp.int32, (n, n), 0)
    col = lax.broadcasted_iota(jnp.int32, (n, n), 1)
    xr = row ^ col
    same = (xr >> (c.bit_length() - 1)) == 0
    incl = jnp.logical_and(same, col <= row)
    strict = jnp.logical_and(same, col < row)

    bg = bg_ref[...]
    tri = jnp.where(incl, 1.0, 0.0).astype(BF16)
    g1, g2, g3 = _split3(bg)
    gc_all = (jnp.dot(tri, g1, preferred_element_type=F32) + jnp.dot(tri, g2, preferred_element_type=F32)
              + jnp.dot(tri, g3, preferred_element_type=F32))
    gc_all_t = gc_all.T

    for h in range(DN_HEADS):
        hs = slice(h * DN_DK, (h + 1) * DN_DK)
        q = q_ref[:, hs]
        k = k_ref[:, hs]
        v = v_ref[:, hs]
        beta = bg[:, h:h + 1]
        gc_col = gc_all[:, DN_HEADS + h:DN_HEADS + h + 1]
        gc_row = gc_all_t[DN_HEADS + h:DN_HEADS + h + 1, :]
        k_t = k.T
        kb = k * beta
        dec = jnp.exp(jnp.where(incl, gc_col - gc_row, -1e30))
        lmat = jnp.where(strict, _dot(kb, k_t) * dec, 0.0)
        qk = jnp.where(incl, _dot(q, k_t) * dec, 0.0)

        nmat = jnp.where((xr >> 1) == 0, -lmat, 0.0)
        for lvl in range(1, c.bit_length() - 1):
            cm = jnp.where((xr >> lvl) == 1, lmat, 0.0)
            x = cm + _dot(nmat, cm)
            nmat = nmat - (x + _dot(x, nmat))

        egc = jnp.exp(gc_col)
        rhs = jnp.concatenate([v * beta, kb * egc], axis=1)
        uw = rhs + _dot(nmat, rhs)
        qg = q * egc

        for i in range(n // c):
            rs = slice(i * c, (i + 1) * c)
            s = s_ref[h]
            wq = _dot(jnp.concatenate([uw[rs, DN_DK:], qg[rs]], axis=0), s)
            v_new = uw[rs, :DN_DK] - wq[:c]
            g_last = gc_row[:, (i + 1) * c - 1:(i + 1) * c]
            kd_t = k_t[:, rs] * jnp.exp(g_last - gc_row[:, rs])
            os_ = _dot(jnp.concatenate([qk[rs, rs], kd_t], axis=0), v_new)
            o = wq[c:] + os_[:c]
            s_ref[h] = s * jnp.exp(g_last) + os_[c:]
            d_ref[rs, hs] = (_rms(o, dn_ref[...]) * _silu(z_ref[rs, hs])).astype(d_ref.dtype)

    @pl.when(t == pl.num_programs(1) - 1)
    def _():
        snew_ref[0] = s_ref[...]


def _gdn_prompt(q, k, v, z, bg, dn, batch, seq):
    n = GDN_BLOCK
    nt = seq // n
    rows = batch * seq
    tok = lambda cols: pl.BlockSpec((n, cols), lambda b, t: (b * nt + t, 0))
    return pl.pallas_call(
        _gdn_prompt_kernel,
        grid=(batch, nt),
        in_specs=[tok(DN_V), tok(DN_V), tok(DN_V), tok(DN_V), tok(LANES), _const_spec((1, DN_DK))],
        out_specs=[tok(DN_V), pl.BlockSpec((1, DN_HEADS, DN_DK, DN_DK), lambda b, t: (b, 0, 0, 0))],
        out_shape=[jax.ShapeDtypeStruct((rows, DN_V), BF16),
                   jax.ShapeDtypeStruct((batch, DN_HEADS, DN_DK, DN_DK), F32)],
        scratch_shapes=[pltpu.VMEM((DN_HEADS, DN_DK, DN_DK), F32)],
        compiler_params=pltpu.CompilerParams(dimension_semantics=("arbitrary", "arbitrary"),
                                             vmem_limit_bytes=VMEM_LIMIT),
        name="gdn_prompt",
    )(q, k, v, z, bg, dn)


def _mem_kv_kernel(mem_ref, g_ref, wk_ref, wv_ref, k_ref, v_ref):
    m = _rms(mem_ref[...], g_ref[...]).astype(BF16)
    k_ref[...] = jnp.dot(m, wk_ref[...], preferred_element_type=F32)
    v_ref[...] = jnp.dot(m, wv_ref[...], preferred_element_type=F32)


def _mem_kv(mem2d, g, wk, wv):
    rows = mem2d.shape[0]
    tok = pl.BlockSpec((N_MEM, D_MODEL), lambda i: (i, 0))
    return pl.pallas_call(
        _mem_kv_kernel,
        grid=(rows // N_MEM,),
        in_specs=[tok, _const_spec((1, D_MODEL)), _const_spec((D_MODEL, D_MODEL)),
                  _const_spec((D_MODEL, D_MODEL))],
        out_specs=[tok, tok],
        out_shape=[jax.ShapeDtypeStruct((rows, D_MODEL), F32)] * 2,
        compiler_params=pltpu.CompilerParams(dimension_semantics=("arbitrary",),
                                             vmem_limit_bytes=VMEM_LIMIT),
        name="mem_kv",
    )(mem2d, g, wk, wv)


def _outproj_kernel(x_ref, c_ref, d_ref, wout_ref, nq_ref, wmq_ref, x1_ref, qm_ref):
    cd = jnp.concatenate([c_ref[...].astype(BF16), d_ref[...].astype(BF16)], axis=1)
    x1 = x_ref[...] + jnp.dot(cd, wout_ref[...], preferred_element_type=F32)
    x1_ref[...] = x1
    hq = _rms(x1, nq_ref[...]).astype(BF16)
    qm_ref[...] = jnp.dot(hq, wmq_ref[...], preferred_element_type=F32).astype(qm_ref.dtype)


def _outproj(x2d, c, d, wout, nq, wmq, tm):
    rows = x2d.shape[0]
    tok = lambda cols: pl.BlockSpec((tm, cols), lambda i: (i, 0))
    return pl.pallas_call(
        _outproj_kernel,
        grid=(rows // tm,),
        in_specs=[tok(D_MODEL), tok(C_CONV), tok(DN_V), _const_spec((D_MODEL, D_MODEL)),
                  _const_spec((1, D_MODEL)), _const_spec((D_MODEL, D_MODEL))],
        out_specs=[tok(D_MODEL), tok(D_MODEL)],
        out_shape=[jax.ShapeDtypeStruct((rows, D_MODEL), F32),
                   jax.ShapeDtypeStruct((rows, D_MODEL), BF16)],
        compiler_params=pltpu.CompilerParams(dimension_semantics=("arbitrary",),
                                             vmem_limit_bytes=VMEM_LIMIT),
        name="outproj",
    )(x2d, c, d, wout, nq, wmq)


def _softmax_rows(s):
    e = jnp.exp(s - jnp.max(s, axis=-1, keepdims=True))
    return e / jnp.sum(e, axis=-1, keepdims=True)


def _attn_prompt_kernel(q_ref, k_ref, v_ref, o_ref):
    for h in range(MEM_HEADS):
        hs = slice(h * MEM_HD, (h + 1) * MEM_HD)
        s = _dot_nt(q_ref[:, hs], k_ref[:, hs]) * (MEM_HD ** -0.5)
        o_ref[:, hs] = _dot(_softmax_rows(s), v_ref[:, hs]).astype(o_ref.dtype)


def _attn_prompt(qm, mk, mv, batch, seq):
    tq = TQ_ATT
    nt = seq // tq
    tok = pl.BlockSpec((tq, D_MODEL), lambda b, t: (b * nt + t, 0))
    kv = pl.BlockSpec((N_MEM, D_MODEL), lambda b, t: (b, 0))
    return pl.pallas_call(
        _attn_prompt_kernel,
        grid=(batch, nt),
        in_specs=[tok, kv, kv],
        out_specs=tok,
        out_shape=jax.ShapeDtypeStruct(qm.shape, BF16),
        compiler_params=pltpu.CompilerParams(dimension_semantics=("arbitrary", "arbitrary"),
                                             vmem_limit_bytes=VMEM_LIMIT),
        name="attn_prompt",
    )(qm, mk, mv)


def _attn_sample_kernel(q_ref, k_ref, v_ref, o_ref):
    q = q_ref[...]
    nb = q.shape[0]
    rid = lax.broadcasted_iota(jnp.int32, (nb, MEM_HD), 0)
    acc = jnp.zeros((nb, MEM_HD), F32)
    for r in range(nb):
        s = _dot_nt(q, k_ref[r]) * (MEM_HD ** -0.5)
        o = _dot(_softmax_rows(s), v_ref[r])
        acc = jnp.where(rid == r, o, acc)
    o_ref[...] = acc.astype(o_ref.dtype)


def _attn_sample(qm, ck, cv):
    nb = SB_ATT
    batch = qm.shape[0]
    qs = pl.BlockSpec((nb, MEM_HD), lambda i, h: (i, h))
    kv = pl.BlockSpec((nb, N_MEM, MEM_HD), lambda i, h: (i, 0, h))
    return pl.pallas_call(
        _attn_sample_kernel,
        grid=(batch // nb, MEM_HEADS),
        in_specs=[qs, kv, kv],
        out_specs=qs,
        out_shape=jax.ShapeDtypeStruct(qm.shape, BF16),
        compiler_params=pltpu.CompilerParams(dimension_semantics=("arbitrary", "arbitrary"),
                                             vmem_limit_bytes=VMEM_LIMIT),
        name="attn_sample",
    )(qm, ck, cv)


def _ffn_kernel(x1_ref, o_ref, wmo_ref, nffn_ref, wg_ref, wu_ref, wd_ref, nf_ref, y_ref, act_ref):
    x2 = x1_ref[...] + jnp.dot(o_ref[...], wmo_ref[...], preferred_element_type=F32)
    h = _rms(x2, nffn_ref[...]).astype(BF16)
    for j in range(D_FF // FF_BLK):
        cs = slice(j * FF_BLK, (j + 1) * FF_BLK)
        gate = jnp.dot(h, wg_ref[:, cs], preferred_element_type=F32)
        up = jnp.dot(h, wu_ref[:, cs], preferred_element_type=F32)
        act_ref[:, cs] = (_silu(gate) * up).astype(BF16)
    x3 = x2 + jnp.dot(act_ref[...], wd_ref[...], preferred_element_type=F32)
    y_ref[...] = _rms(x3, nf_ref[...])


def _ffn(x1, o, wmo, nffn, wg, wu, wd, nf, tm):
    rows = x1.shape[0]
    tok = pl.BlockSpec((tm, D_MODEL), lambda i: (i, 0))
    return pl.pallas_call(
        _ffn_kernel,
        grid=(rows // tm,),
        in_specs=[tok, tok, _const_spec((D_MODEL, D_MODEL)), _const_spec((1, D_MODEL)),
                  _const_spec((D_MODEL, D_FF)), _const_spec((D_MODEL, D_FF)), _const_spec((D_FF, D_MODEL)),
                  _const_spec((1, D_MODEL))],
        out_specs=tok,
        out_shape=jax.ShapeDtypeStruct((rows, D_MODEL), F32),
        scratch_shapes=[pltpu.VMEM((tm, D_FF), BF16)],
        compiler_params=pltpu.CompilerParams(dimension_semantics=("arbitrary",),
                                             vmem_limit_bytes=VMEM_LIMIT),
        name="ffn",
    )(x1, o, wmo, nffn, wg, wu, wd, nf)


def _inproj_sample_kernel(x_ref, nmix_ref, wmain_ref, wbd_ref, convw_ref, convb_ref, lng_ref, lnb_ref,
                          scw_ref, alog_ref, dtb_ref, cache_ref, state_ref,
                          c_ref, u_ref, pq_ref, kt_ref, qt_ref, v_ref, z_ref, aux_ref):
    h = _rms(x_ref[...], nmix_ref[...]).astype(BF16)
    pg = jnp.dot(h, wmain_ref[:, 0:O_QKV], preferred_element_type=F32)
    u = pg[:, :C_CONV] * _sigmoid(pg[:, C_CONV:])
    u_ref[...] = u
    pq = jnp.dot(h, wmain_ref[:, O_QKV:O_Z], preferred_element_type=F32)
    pq_ref[...] = pq
    z_ref[...] = jnp.dot(h, wmain_ref[:, O_Z:MAIN_COLS], preferred_element_type=F32)
    bg = _beta_g(jnp.dot(h, wbd_ref[...], preferred_element_type=F32), alog_ref[...], dtb_ref[...])

    acc = convw_ref[CONV_WIDTH - 1:CONV_WIDTH, :] * u
    for j in range(CONV_WIDTH - 1):
        acc = acc + convw_ref[j:j + 1, :] * cache_ref[j]
    cn = _layer_norm(acc + convb_ref[...], lng_ref[...], lnb_ref[...])
    c_ref[...] = _silu(cn).astype(c_ref.dtype)

    acc = scw_ref[SHORT_CONV - 1:SHORT_CONV, :] * pq
    for j in range(SHORT_CONV - 1):
        acc = acc + scw_ref[j:j + 1, :] * state_ref[j]
    q, k, v = _qkv_heads(_silu(acc))
    v_ref[...] = v

    lane = lax.broadcasted_iota(jnp.int32, bg.shape, 1)
    aux = bg
    for hh in range(DN_HEADS):
        hs = slice(hh * DN_DK, (hh + 1) * DN_DK)
        kt_ref[hh] = k[:, hs].T
        qt_ref[hh] = q[:, hs].T
        qk = jnp.sum(q[:, hs] * k[:, hs], axis=-1, keepdims=True)
        aux = jnp.where(lane == 2 * DN_HEADS + hh, qk, aux)
    aux_ref[...] = aux


def _inproj_sample(x2d, nmix, wmain, wbd, convw, convb, lng, lnb, scw, alog, dtb, cache_t, state_t):
    b = x2d.shape[0]
    args = (x2d, nmix, wmain, wbd, convw, convb, lng, lnb, scw, alog, dtb, cache_t, state_t)
    return pl.pallas_call(
        _inproj_sample_kernel,
        grid=(1,),
        in_specs=[_const_spec(a.shape) for a in args],
        out_specs=[_full_spec((b, C_CONV)), _full_spec((b, C_CONV)), _full_spec((b, QKV_COLS)),
                   _full_spec((DN_HEADS, DN_DK, b)), _full_spec((DN_HEADS, DN_DK, b)),
                   _full_spec((b, DN_V)), _full_spec((b, DN_V)), _full_spec((b, LANES))],
        out_shape=[jax.ShapeDtypeStruct((b, C_CONV), BF16),
                   jax.ShapeDtypeStruct((b, C_CONV), F32),
                   jax.ShapeDtypeStruct((b, QKV_COLS), F32),
                   jax.ShapeDtypeStruct((DN_HEADS, DN_DK, b), F32),
                   jax.ShapeDtypeStruct((DN_HEADS, DN_DK, b), F32),
                   jax.ShapeDtypeStruct((b, DN_V), F32),
                   jax.ShapeDtypeStruct((b, DN_V), F32),
                   jax.ShapeDtypeStruct((b, LANES), F32)],
        compiler_params=pltpu.CompilerParams(dimension_semantics=("arbitrary",),
                                             vmem_limit_bytes=VMEM_LIMIT),
        name="inproj_sample",
    )(*args)


def _gdn_sample_kernel(s_ref, kt_ref, qt_ref, v_ref, z_ref, aux_ref, dn_ref, snew_ref, d_ref):
    i = pl.program_id(0)
    nb = s_ref.shape[0]
    total = kt_ref.shape[2]

    def body(bl, carry):
        shift = (total - (i * nb + bl)) % total
        for h in range(DN_HEADS):
            hs = slice(h * DN_DK, (h + 1) * DN_DK)
            kcol = pltpu.roll(kt_ref[h], shift, axis=1)[:, 0:1]
            qcol = pltpu.roll(qt_ref[h], shift, axis=1)[:, 0:1]
            aux = aux_ref[bl]
            beta = aux[:, h:h + 1]
            eg = jnp.exp(aux[:, DN_HEADS + h:DN_HEADS + h + 1])
            qk = aux[:, 2 * DN_HEADS + h:2 * DN_HEADS + h + 1]
            s = s_ref[bl, h]
            ks = jnp.sum(kcol * s, axis=0, keepdims=True)
            qs = jnp.sum(qcol * s, axis=0, keepdims=True)
            v_new = beta * (v_ref[bl, :, hs] - eg * ks)
            o = eg * qs + qk * v_new
            snew_ref[bl, h] = s * eg + kcol * v_new
            d_ref[bl, :, hs] = (_rms(o, dn_ref[...]) * _silu(z_ref[bl, :, hs])).astype(d_ref.dtype)
        return carry

    lax.fori_loop(0, nb, body, 0)


def _gdn_sample(s, kt, qt, v, z, aux, dn):
    batch = s.shape[0]
    nb = SB_GDN
    sblk = pl.BlockSpec((nb, DN_HEADS, DN_DK, DN_DK), lambda i: (i, 0, 0, 0))
    rowblk = lambda cols: pl.BlockSpec((nb, 1, cols), lambda i: (i, 0, 0))
    as_rows = lambda a: a.reshape(batch, 1, a.shape[-1])
    snew, d = pl.pallas_call(
        _gdn_sample_kernel,
        grid=(batch // nb,),
        in_specs=[sblk, _const_spec(kt.shape), _const_spec(qt.shape), rowblk(DN_V), rowblk(DN_V),
                  rowblk(LANES), _const_spec((1, DN_DK))],
        out_specs=[sblk, rowblk(DN_V)],
        out_shape=[jax.ShapeDtypeStruct(s.shape, F32), jax.ShapeDtypeStruct((batch, 1, DN_V), F32)],
        compiler_params=pltpu.CompilerParams(dimension_semantics=("arbitrary",),
                                             vmem_limit_bytes=VMEM_LIMIT),
        name="gdn_sample",
    )(s, kt, qt, as_rows(v), as_rows(z), as_rows(aux), dn)
    return snew, d.reshape(batch, DN_V)


def _pad_lanes(vec, offset):
    return jnp.zeros((1, LANES), F32).at[0, offset:offset + vec.shape[0]].set(vec.astype(F32))


def kernel(x_prompt, x_sample, mem_prompt, cache_conv, state_short_conv, state_delta, cache_mem_k, cache_mem_v,
           norm_mix, w_in, conv_w, conv_b, conv_ln_g, conv_ln_b, sc_w, a_log, dt_bias, dn_norm, w_out,
           norm_mem_q, norm_mem_kv, w_mq, w_mk, w_mv, w_mo, norm_ffn, w_gate, w_up, w_down, norm_f):
    depth = w_in.shape[0]
    assert depth == 1, "single-layer stack"
    bp, seq, _ = x_prompt.shape
    bs = x_sample.shape[0]
    l = 0

    row = lambda a: a.reshape(1, -1).astype(F32)
    wmain = w_in[l][:, :MAIN_COLS].astype(BF16)
    wbd = jnp.pad(w_in[l][:, MAIN_COLS:], ((0, 0), (0, LANES - 2 * DN_HEADS))).astype(BF16)
    alog = _pad_lanes(a_log[l], DN_HEADS)
    dtb = _pad_lanes(dt_bias[l], DN_HEADS)
    mixer_w = (row(norm_mix[l]), wmain, wbd, conv_w[l], row(conv_b[l]), row(conv_ln_g[l]), row(conv_ln_b[l]),
               sc_w[l], alog, dtb)
    wout = w_out[l].astype(BF16)
    wmq = w_mq[l].astype(BF16)
    wmo = w_mo[l].astype(BF16)
    ffn_w = (wmo, row(norm_ffn[l]), w_gate[l].astype(BF16), w_up[l].astype(BF16), w_down[l].astype(BF16),
             row(norm_f))
    dn = row(dn_norm[l])

    xp = x_prompt.reshape(bp * seq, D_MODEL)
    mk, mv = _mem_kv(mem_prompt.reshape(bp * N_MEM, D_MODEL), row(norm_mem_kv[l]),
                     w_mk[l].astype(BF16), w_mv[l].astype(BF16))
    c_p, q_p, k_p, v_p, z_p, bg_p, nconv_p, nsc_p = _inproj_prompt(xp, bp, seq, *mixer_w)
    d_p, ndelta_p = _gdn_prompt(q_p, k_p, v_p, z_p, bg_p, dn, bp, seq)
    x1_p, qm_p = _outproj(xp, c_p, d_p, wout, row(norm_mem_q[l]), wmq, TM_O)
    o_p = _attn_prompt(qm_p, mk, mv, bp, seq)
    y_p = _ffn(x1_p, o_p, *ffn_w, TM_F)

    xs = x_sample.reshape(bs, D_MODEL)
    cache_t = jnp.swapaxes(cache_conv[l], 0, 1)
    state_t = jnp.swapaxes(state_short_conv[l], 0, 1)
    c_s, u_s, pq_s, kt_s, qt_s, v_s, z_s, aux_s = _inproj_sample(xs, *mixer_w, cache_t, state_t)
    ndelta_s, d_s = _gdn_sample(state_delta[l], kt_s, qt_s, v_s, z_s, aux_s, dn)
    x1_s, qm_s = _outproj(xs, c_s, d_s, wout, row(norm_mem_q[l]), wmq, bs)
    o_s = _attn_sample(qm_s, cache_mem_k[l].reshape(bs, N_MEM, D_MODEL),
                       cache_mem_v[l].reshape(bs, N_MEM, D_MODEL))
    y_s = _ffn(x1_s, o_s, *ffn_w, bs)
    nconv_s = jnp.concatenate([cache_conv[l][:, 1:], u_s[:, None, :]], axis=1)
    nsc_s = jnp.concatenate([state_short_conv[l][:, 1:], pq_s[:, None, :]], axis=1)

    return (y_p.reshape(bp, seq, D_MODEL), y_s.reshape(bs, 1, D_MODEL),
            nconv_p[None], nsc_p[None], ndelta_p[None],
            mk.reshape(1, bp, N_MEM, MEM_HEADS, MEM_HD), mv.reshape(1, bp, N_MEM, MEM_HEADS, MEM_HD),
            nconv_s[None], nsc_s[None], ndelta_s[None])
```

```python
import functools

import jax
import jax.numpy as jnp
from jax import lax
from jax.experimental import pallas as pl
from jax.experimental.pallas import tpu as pltpu

F32 = jnp.float32
BF16 = jnp.bfloat16

D_MODEL = 1024
C_CONV = 512
CONV_WIDTH = 31
DN_HEADS = 4
DN_DK = 128
DN_V = 512
QKV_COLS = 1536
SHORT_CONV = 4
N_MEM = 256
MEM_HEADS = 4
MEM_HD = 256
D_FF = 2816
MAIN_COLS = 3072
O_QKV = 1024
O_Z = 2560

LANES = 128
SUBLANES = 8
VMEM_LIMIT = 52 * 1024 * 1024

TM_A = 512
U_HALO = 32
Q_HALO = 8
CONV_ROWS = 64
GDN_STEP = 512
GDN_BLOCK = 256
GDN_CHUNK = 128
TM_O = 512
TQ_ATT = 512
TM_F = 512
FF_BLK = 256
SB_GDN = 16
SB_ATT = 4


def _dot(a, b):
    return jnp.dot(a.astype(BF16), b.astype(BF16), preferred_element_type=F32)


def _dot_nt(a, b):
    return lax.dot_general(a.astype(BF16), b.astype(BF16), (((1,), (1,)), ((), ())),
                           preferred_element_type=F32)


def _sigmoid(x):
    return 1.0 / (1.0 + jnp.exp(-x))


def _silu(x):
    return x * _sigmoid(x)


def _softplus(x):
    return jnp.maximum(x, 0.0) + jnp.log1p(jnp.exp(-jnp.abs(x)))


def _rms(x, g, eps=1e-6):
    return x * lax.rsqrt(jnp.mean(x * x, axis=-1, keepdims=True) + eps) * g


def _layer_norm(x, g, b, eps=1e-5):
    mu = jnp.mean(x, axis=-1, keepdims=True)
    xc = x - mu
    return xc * lax.rsqrt(jnp.mean(xc * xc, axis=-1, keepdims=True) + eps) * g + b


def _qkv_heads(qkv):
    qs, ks = [], []
    for h in range(DN_HEADS):
        q = qkv[:, h * DN_DK:(h + 1) * DN_DK]
        k = qkv[:, DN_V + h * DN_DK:DN_V + (h + 1) * DN_DK]
        qs.append(q * lax.rsqrt(jnp.sum(q * q, axis=-1, keepdims=True) + 1e-6) * (DN_DK ** -0.5))
        ks.append(k * lax.rsqrt(jnp.sum(k * k, axis=-1, keepdims=True) + 1e-6))
    return jnp.concatenate(qs, axis=1), jnp.concatenate(ks, axis=1), qkv[:, 2 * DN_V:]


def _beta_g(pbd, alog, dtb):
    lane = lax.broadcasted_iota(jnp.int32, pbd.shape, 1)
    g = -jnp.exp(alog) * _softplus(pbd + dtb)
    return jnp.where(lane < DN_HEADS, _sigmoid(pbd), g)


def _const_spec(shape):
    zeros = (0,) * len(shape)
    return pl.BlockSpec(shape, lambda *_: zeros, pipeline_mode=pl.Buffered(1))


def _full_spec(shape):
    zeros = (0,) * len(shape)
    return pl.BlockSpec(shape, lambda *_: zeros)


def _causal_conv_rows(buf, w_ref, start, width, rows):
    out = None
    for s in range(SUBLANES):
        taps = [j for j in range(width) if (start + j) % SUBLANES == s]
        if not taps:
            continue
        win = rows if s == 0 else rows + SUBLANES
        part = None
        for j in taps:
            base = start + j - s
            assert base >= 0 and base + win <= buf.shape[0]
            term = w_ref[j:j + 1, :] * buf[base:base + win, :]
            part = term if part is None else part + term
        if s:
            part = pltpu.roll(part, win - s, axis=0)[:rows]
        out = part if out is None else out + part
    return out


def _inproj_prompt_kernel(x_ref, nmix_ref, wmain_ref, wbd_ref, convw_ref, convb_ref, lng_ref, lnb_ref,
                          scw_ref, alog_ref, dtb_ref,
                          c_ref, q_ref, k_ref, v_ref, z_ref, bg_ref, nconv_ref, nsc_ref,
                          ubuf, qbuf):
    t = pl.program_id(1)
    tm = x_ref.shape[0]

    @pl.when(t == 0)
    def _():
        ubuf[0:U_HALO, :] = jnp.zeros((U_HALO, C_CONV), F32)
        qbuf[0:Q_HALO, :] = jnp.zeros((Q_HALO, QKV_COLS), F32)

    h = _rms(x_ref[...], nmix_ref[...]).astype(BF16)

    pg = jnp.dot(h, wmain_ref[:, 0:O_QKV], preferred_element_type=F32)
    ubuf[U_HALO:U_HALO + tm, :] = pg[:, :C_CONV] * _sigmoid(pg[:, C_CONV:])
    qbuf[Q_HALO:Q_HALO + tm, :] = jnp.dot(h, wmain_ref[:, O_QKV:O_Z], preferred_element_type=F32)
    z_ref[...] = jnp.dot(h, wmain_ref[:, O_Z:MAIN_COLS], preferred_element_type=F32)
    pbd = jnp.dot(h, wbd_ref[...], preferred_element_type=F32)
    bg_ref[...] = _beta_g(pbd, alog_ref[...], dtb_ref[...])

    off_u = U_HALO - (CONV_WIDTH - 1)
    for rb in range(tm // CONV_ROWS):
        r0 = rb * CONV_ROWS
        acc = _causal_conv_rows(ubuf, convw_ref, r0 + off_u, CONV_WIDTH, CONV_ROWS)
        cn = _layer_norm(acc + convb_ref[...], lng_ref[...], lnb_ref[...])
        c_ref[r0:r0 + CONV_ROWS, :] = _silu(cn).astype(c_ref.dtype)

    off_q = Q_HALO - (SHORT_CONV - 1)
    for rb in range(tm // CONV_ROWS):
        r0 = rb * CONV_ROWS
        acc = _causal_conv_rows(qbuf, scw_ref, r0 + off_q, SHORT_CONV, CONV_ROWS)
        q, k, v = _qkv_heads(_silu(acc))
        q_ref[r0:r0 + CONV_ROWS, :] = q
        k_ref[r0:r0 + CONV_ROWS, :] = k
        v_ref[r0:r0 + CONV_ROWS, :] = v

    @pl.when(t == pl.num_programs(1) - 1)
    def _():
        nconv_ref[0] = ubuf[tm + off_u:tm + U_HALO, :]
        nsc_ref[0] = qbuf[tm + off_q:tm + Q_HALO, :]

    ubuf[0:U_HALO, :] = ubuf[tm:tm + U_HALO, :]
    qbuf[0:Q_HALO, :] = qbuf[tm:tm + Q_HALO, :]


def _inproj_prompt(x2d, batch, seq, nmix, wmain, wbd, convw, convb, lng, lnb, scw, alog, dtb):
    tm = TM_A
    nt = seq // tm
    rows = batch * seq
    tok = lambda cols: pl.BlockSpec((tm, cols), lambda b, t: (b * nt + t, 0))
    return pl.pallas_call(
        _inproj_prompt_kernel,
        grid=(batch, nt),
        in_specs=[tok(D_MODEL), _const_spec((1, D_MODEL)), _const_spec((D_MODEL, MAIN_COLS)),
                  _const_spec((D_MODEL, LANES)), _const_spec((CONV_WIDTH, C_CONV)),
                  _const_spec((1, C_CONV)), _const_spec((1, C_CONV)), _const_spec((1, C_CONV)),
                  _const_spec((SHORT_CONV, QKV_COLS)), _const_spec((1, LANES)), _const_spec((1, LANES))],
        out_specs=[tok(C_CONV), tok(DN_V), tok(DN_V), tok(DN_V), tok(DN_V), tok(LANES),
                   pl.BlockSpec((1, CONV_WIDTH - 1, C_CONV), lambda b, t: (b, 0, 0)),
                   pl.BlockSpec((1, SHORT_CONV - 1, QKV_COLS), lambda b, t: (b, 0, 0))],
        out_shape=[jax.ShapeDtypeStruct((rows, C_CONV), BF16),
                   jax.ShapeDtypeStruct((rows, DN_V), F32),
                   jax.ShapeDtypeStruct((rows, DN_V), F32),
                   jax.ShapeDtypeStruct((rows, DN_V), F32),
                   jax.ShapeDtypeStruct((rows, DN_V), F32),
                   jax.ShapeDtypeStruct((rows, LANES), F32),
                   jax.ShapeDtypeStruct((batch, CONV_WIDTH - 1, C_CONV), F32),
                   jax.ShapeDtypeStruct((batch, SHORT_CONV - 1, QKV_COLS), F32)],
        scratch_shapes=[pltpu.VMEM((U_HALO + tm, C_CONV), F32), pltpu.VMEM((Q_HALO + tm, QKV_COLS), F32)],
        compiler_params=pltpu.CompilerParams(dimension_semantics=("arbitrary", "arbitrary"),
                                             vmem_limit_bytes=VMEM_LIMIT),
        name="inproj_prompt",
    )(x2d, nmix, wmain, wbd, convw, convb, lng, lnb, scw, alog, dtb)


def _split3(x):
    x1 = x.astype(BF16)
    r1 = x - x1.astype(F32)
    x2 = r1.astype(BF16)
    x3 = (r1 - x2.astype(F32)).astype(BF16)
    return x1, x2, x3


def _gdn_prompt_kernel(q_ref, k_ref, v_ref, z_ref, bg_ref, dn_ref, d_ref, snew_ref, s_ref):
    t = pl.program_id(1)
    n = GDN_BLOCK
    c = GDN_CHUNK

    @pl.when(t == 0)
    def _():
        s_ref[...] = jnp.zeros(s_ref.shape, F32)

    row = lax.broadcasted_iota(jnp.int32, (n, n), 0)
    col = lax.broadcasted_iota(jnp.int32, (n, n), 1)
    xr = row ^ col
    same = (xr >> (c.bit_length() - 1)) == 0
    incl = jnp.logical_and(same, col <= row)
    strict = jnp.logical_and(same, col < row)

    tri = jnp.where(incl, 1.0, 0.0).astype(BF16)

    chains = []
    for j in range(q_ref.shape[0] // n):
        bs = slice(j * n, (j + 1) * n)
        bg = bg_ref[bs, :]
        g1, g2, g3 = _split3(bg)
        gc_all = (jnp.dot(tri, g1, preferred_element_type=F32) + jnp.dot(tri, g2, preferred_element_type=F32)
                  + jnp.dot(tri, g3, preferred_element_type=F32))
        gc_all_t = gc_all.T
        for h in range(DN_HEADS):
            hs = slice(h * DN_DK, (h + 1) * DN_DK)
            q = q_ref[bs, hs]
            k = k_ref[bs, hs]
            beta = bg[:, h:h + 1]
            gc_col = gc_all[:, DN_HEADS + h:DN_HEADS + h + 1]
            gc_row = gc_all_t[DN_HEADS + h:DN_HEADS + h + 1, :]
            k_t = k.T
            kb = k * beta
            dec = jnp.exp(jnp.where(incl, gc_col - gc_row, -1e30))
            lmat = jnp.where(strict, _dot(kb, k_t) * dec, 0.0)
            egc = jnp.exp(gc_col)
            chains.append(dict(
                j=j, h=h, k_t=k_t, gc_row=gc_row, lmat=lmat,
                qk=jnp.where(incl, _dot(q, k_t) * dec, 0.0),
                rhs=jnp.concatenate([v_ref[bs, hs] * beta, kb * egc], axis=1),
                qg=q * egc,
                nmat=jnp.where((xr >> 1) == 0, -lmat, 0.0)))

    for lvl in range(1, c.bit_length() - 1):
        for ch in chains:
            cm = jnp.where((xr >> lvl) == 1, ch["lmat"], 0.0)
            x = cm + _dot(ch["nmat"], cm)
            ch["nmat"] = ch["nmat"] - (x + _dot(x, ch["nmat"]))
    for ch in chains:
        ch["uw"] = ch["rhs"] + _dot(ch["nmat"], ch["rhs"])

    for ch in chains:
        h = ch["h"]
        hs = slice(h * DN_DK, (h + 1) * DN_DK)
        for i in range(n // c):
            rs = slice(i * c, (i + 1) * c)
            ts = slice(ch["j"] * n + i * c, ch["j"] * n + (i + 1) * c)
            s = s_ref[h]
            wq = _dot(jnp.concatenate([ch["uw"][rs, DN_DK:], ch["qg"][rs]], axis=0), s)
            v_new = ch["uw"][rs, :DN_DK] - wq[:c]
            g_last = ch["gc_row"][:, (i + 1) * c - 1:(i + 1) * c]
            kd_t = ch["k_t"][:, rs] * jnp.exp(g_last - ch["gc_row"][:, rs])
            os_ = _dot(jnp.concatenate([ch["qk"][rs, rs], kd_t], axis=0), v_new)
            o = wq[c:] + os_[:c]
            s_ref[h] = s * jnp.exp(g_last) + os_[c:]
            d_ref[ts, hs] = (_rms(o, dn_ref[...]) * _silu(z_ref[ts, hs])).astype(d_ref.dtype)

    @pl.when(t == pl.num_programs(1) - 1)
    def _():
        snew_ref[0] = s_ref[...]


def _gdn_prompt(q, k, v, z, bg, dn, batch, seq):
    n = GDN_STEP
    nt = seq // n
    rows = batch * seq
    tok = lambda cols: pl.BlockSpec((n, cols), lambda b, t: (b * nt + t, 0))
    return pl.pallas_call(
        _gdn_prompt_kernel,
        grid=(batch, nt),
        in_specs=[tok(DN_V), tok(DN_V), tok(DN_V), tok(DN_V), tok(LANES), _const_spec((1, DN_DK))],
        out_specs=[tok(DN_V), pl.BlockSpec((1, DN_HEADS, DN_DK, DN_DK), lambda b, t: (b, 0, 0, 0))],
        out_shape=[jax.ShapeDtypeStruct((rows, DN_V), BF16),
                   jax.ShapeDtypeStruct((batch, DN_HEADS, DN_DK, DN_DK), F32)],
        scratch_shapes=[pltpu.VMEM((DN_HEADS, DN_DK, DN_DK), F32)],
        compiler_params=pltpu.CompilerParams(dimension_semantics=("arbitrary", "arbitrary"),
                                             vmem_limit_bytes=VMEM_LIMIT),
        name="gdn_prompt",
    )(q, k, v, z, bg, dn)


def _mem_kv_kernel(mem_ref, g_ref, wk_ref, wv_ref, k_ref, v_ref):
    m = _rms(mem_ref[...], g_ref[...]).astype(BF16)
    k_ref[...] = jnp.dot(m, wk_ref[...], preferred_element_type=F32)
    v_ref[...] = jnp.dot(m, wv_ref[...], preferred_element_type=F32)


def _mem_kv(mem2d, g, wk, wv):
    rows = mem2d.shape[0]
    tok = pl.BlockSpec((N_MEM, D_MODEL), lambda i: (i, 0))
    return pl.pallas_call(
        _mem_kv_kernel,
        grid=(rows // N_MEM,),
        in_specs=[tok, _const_spec((1, D_MODEL)), _const_spec((D_MODEL, D_MODEL)),
                  _const_spec((D_MODEL, D_MODEL))],
        out_specs=[tok, tok],
        out_shape=[jax.ShapeDtypeStruct((rows, D_MODEL), F32)] * 2,
        compiler_params=pltpu.CompilerParams(dimension_semantics=("arbitrary",),
                                             vmem_limit_bytes=VMEM_LIMIT),
        name="mem_kv",
    )(mem2d, g, wk, wv)


def _outproj_kernel(x_ref, c_ref, d_ref, wout_ref, nq_ref, wmq_ref, x1_ref, qm_ref):
    cd = jnp.concatenate([c_ref[...].astype(BF16), d_ref[...].astype(BF16)], axis=1)
    x1 = x_ref[...] + jnp.dot(cd, wout_ref[...], preferred_element_type=F32)
    x1_ref[...] = x1
    hq = _rms(x1, nq_ref[...]).astype(BF16)
    qm_ref[...] = jnp.dot(hq, wmq_ref[...], preferred_element_type=F32).astype(qm_ref.dtype)


def _outproj(x2d, c, d, wout, nq, wmq, tm, qm_dtype):
    rows = x2d.shape[0]
    tok = lambda cols: pl.BlockSpec((tm, cols), lambda i: (i, 0))
    return pl.pallas_call(
        _outproj_kernel,
        grid=(rows // tm,),
        in_specs=[tok(D_MODEL), tok(C_CONV), tok(DN_V), _const_spec((D_MODEL, D_MODEL)),
                  _const_spec((1, D_MODEL)), _const_spec((D_MODEL, D_MODEL))],
        out_specs=[tok(D_MODEL), tok(D_MODEL)],
        out_shape=[jax.ShapeDtypeStruct((rows, D_MODEL), F32),
                   jax.ShapeDtypeStruct((rows, D_MODEL), qm_dtype)],
        compiler_params=pltpu.CompilerParams(dimension_semantics=("arbitrary",),
                                             vmem_limit_bytes=VMEM_LIMIT),
        name="outproj",
    )(x2d, c, d, wout, nq, wmq)


def _softmax_rows(s):
    e = jnp.exp(s - jnp.max(s, axis=-1, keepdims=True))
    return e / jnp.sum(e, axis=-1, keepdims=True)


def _attn_prompt_kernel(q_ref, k_ref, v_ref, o_ref):
    for h in range(MEM_HEADS):
        hs = slice(h * MEM_HD, (h + 1) * MEM_HD)
        s = _dot_nt(q_ref[:, hs], k_ref[:, hs]) * (MEM_HD ** -0.5)
        o_ref[:, hs] = _dot(_softmax_rows(s), v_ref[:, hs]).astype(o_ref.dtype)


def _attn_prompt(qm, mk, mv, batch, seq):
    tq = TQ_ATT
    nt = seq // tq
    tok = pl.BlockSpec((tq, D_MODEL), lambda b, t: (b * nt + t, 0))
    kv = pl.BlockSpec((N_MEM, D_MODEL), lambda b, t: (b, 0))
    return pl.pallas_call(
        _attn_prompt_kernel,
        grid=(batch, nt),
        in_specs=[tok, kv, kv],
        out_specs=tok,
        out_shape=jax.ShapeDtypeStruct(qm.shape, BF16),
        compiler_params=pltpu.CompilerParams(dimension_semantics=("arbitrary", "arbitrary"),
                                             vmem_limit_bytes=VMEM_LIMIT),
        name="attn_prompt",
    )(qm, mk, mv)


def _attn_sample_kernel(q_ref, k_ref, v_ref, o_ref):
    for r in range(q_ref.shape[0]):
        s = jnp.sum(k_ref[r] * q_ref[r][None], axis=-1, keepdims=True) * (MEM_HD ** -0.5)
        e = jnp.exp(s - jnp.max(s, axis=0, keepdims=True))
        p = e / jnp.sum(e, axis=0, keepdims=True)
        o_ref[r] = jnp.sum(p * v_ref[r], axis=0)


def _attn_sample(qm, ck, cv):
    nb = SB_ATT
    batch = qm.shape[0]
    qs = pl.BlockSpec((nb, MEM_HEADS, MEM_HD), lambda i: (i, 0, 0))
    kv = pl.BlockSpec((nb, N_MEM, MEM_HEADS, MEM_HD), lambda i: (i, 0, 0, 0))
    return pl.pallas_call(
        _attn_sample_kernel,
        grid=(batch // nb,),
        in_specs=[qs, kv, kv],
        out_specs=qs,
        out_shape=jax.ShapeDtypeStruct(qm.shape, F32),
        compiler_params=pltpu.CompilerParams(dimension_semantics=("arbitrary",),
                                             vmem_limit_bytes=VMEM_LIMIT),
        name="attn_sample",
    )(qm, ck, cv)


def _ffn_kernel(x1_ref, o_ref, wmo_ref, nffn_ref, wg_ref, wu_ref, wd_ref, nf_ref, y_ref, act_ref):
    x2 = x1_ref[...] + _dot(o_ref[...], wmo_ref[...])
    h = _rms(x2, nffn_ref[...]).astype(BF16)
    for j in range(D_FF // FF_BLK):
        cs = slice(j * FF_BLK, (j + 1) * FF_BLK)
        gate = jnp.dot(h, wg_ref[:, cs], preferred_element_type=F32)
        up = jnp.dot(h, wu_ref[:, cs], preferred_element_type=F32)
        act_ref[:, cs] = (_silu(gate) * up).astype(BF16)
    x3 = x2 + jnp.dot(act_ref[...], wd_ref[...], preferred_element_type=F32)
    y_ref[...] = _rms(x3, nf_ref[...])


def _ffn(x1, o, wmo, nffn, wg, wu, wd, nf, tm):
    rows = x1.shape[0]
    tok = pl.BlockSpec((tm, D_MODEL), lambda i: (i, 0))
    return pl.pallas_call(
        _ffn_kernel,
        grid=(rows // tm,),
        in_specs=[tok, tok, _const_spec((D_MODEL, D_MODEL)), _const_spec((1, D_MODEL)),
                  _const_spec((D_MODEL, D_FF)), _const_spec((D_MODEL, D_FF)), _const_spec((D_FF, D_MODEL)),
                  _const_spec((1, D_MODEL))],
        out_specs=tok,
        out_shape=jax.ShapeDtypeStruct((rows, D_MODEL), F32),
        scratch_shapes=[pltpu.VMEM((tm, D_FF), BF16)],
        compiler_params=pltpu.CompilerParams(dimension_semantics=("arbitrary",),
                                             vmem_limit_bytes=VMEM_LIMIT),
        name="ffn",
    )(x1, o, wmo, nffn, wg, wu, wd, nf)


def _inproj_sample_kernel(x_ref, nmix_ref, wmain_ref, wbd_ref, convw_ref, convb_ref, lng_ref, lnb_ref,
                          scw_ref, alog_ref, dtb_ref, cache_ref, state_ref,
                          c_ref, u_ref, pq_ref, kt_ref, qt_ref, v_ref, z_ref, aux_ref):
    h = _rms(x_ref[...], nmix_ref[...]).astype(BF16)
    pg = jnp.dot(h, wmain_ref[:, 0:O_QKV], preferred_element_type=F32)
    u = pg[:, :C_CONV] * _sigmoid(pg[:, C_CONV:])
    u_ref[...] = u
    pq = jnp.dot(h, wmain_ref[:, O_QKV:O_Z], preferred_element_type=F32)
    pq_ref[...] = pq
    z_ref[...] = jnp.dot(h, wmain_ref[:, O_Z:MAIN_COLS], preferred_element_type=F32)
    bg = _beta_g(jnp.dot(h, wbd_ref[...], preferred_element_type=F32), alog_ref[...], dtb_ref[...])

    acc = convw_ref[CONV_WIDTH - 1:CONV_WIDTH, :] * u
    for j in range(CONV_WIDTH - 1):
        acc = acc + convw_ref[j:j + 1, :] * cache_ref[j]
    cn = _layer_norm(acc + convb_ref[...], lng_ref[...], lnb_ref[...])
    c_ref[...] = _silu(cn).astype(c_ref.dtype)

    acc = scw_ref[SHORT_CONV - 1:SHORT_CONV, :] * pq
    for j in range(SHORT_CONV - 1):
        acc = acc + scw_ref[j:j + 1, :] * state_ref[j]
    q, k, v = _qkv_heads(_silu(acc))
    v_ref[...] = v

    lane = lax.broadcasted_iota(jnp.int32, bg.shape, 1)
    aux = bg
    for hh in range(DN_HEADS):
        hs = slice(hh * DN_DK, (hh + 1) * DN_DK)
        kt_ref[hh] = k[:, hs].T
        qt_ref[hh] = q[:, hs].T
        qk = jnp.sum(q[:, hs] * k[:, hs], axis=-1, keepdims=True)
        aux = jnp.where(lane == 2 * DN_HEADS + hh, qk, aux)
    aux_ref[...] = aux


def _inproj_sample(x2d, nmix, wmain, wbd, convw, convb, lng, lnb, scw, alog, dtb, cache_t, state_t):
    b = x2d.shape[0]
    args = (x2d, nmix, wmain, wbd, convw, convb, lng, lnb, scw, alog, dtb, cache_t, state_t)
    return pl.pallas_call(
        _inproj_sample_kernel,
        grid=(1,),
        in_specs=[_const_spec(a.shape) for a in args],
        out_specs=[_full_spec((b, C_CONV)), _full_spec((b, C_CONV)), _full_spec((b, QKV_COLS)),
                   _full_spec((DN_HEADS, DN_DK, b)), _full_spec((DN_HEADS, DN_DK, b)),
                   _full_spec((b, DN_V)), _full_spec((b, DN_V)), _full_spec((b, LANES))],
        out_shape=[jax.ShapeDtypeStruct((b, C_CONV), BF16),
                   jax.ShapeDtypeStruct((b, C_CONV), F32),
                   jax.ShapeDtypeStruct((b, QKV_COLS), F32),
                   jax.ShapeDtypeStruct((DN_HEADS, DN_DK, b), F32),
                   jax.ShapeDtypeStruct((DN_HEADS, DN_DK, b), F32),
                   jax.ShapeDtypeStruct((b, DN_V), F32),
                   jax.ShapeDtypeStruct((b, DN_V), F32),
                   jax.ShapeDtypeStruct((b, LANES), F32)],
        compiler_params=pltpu.CompilerParams(dimension_semantics=("arbitrary",),
                                             vmem_limit_bytes=VMEM_LIMIT),
        name="inproj_sample",
    )(*args)


def _gdn_sample_kernel(s_ref, kt_ref, qt_ref, v_ref, z_ref, aux_ref, dn_ref, snew_ref, d_ref):
    i = pl.program_id(0)
    nb = s_ref.shape[0]
    total = kt_ref.shape[2]

    def body(bl, carry):
        shift = (total - (i * nb + bl)) % total
        for h in range(DN_HEADS):
            hs = slice(h * DN_DK, (h + 1) * DN_DK)
            kcol = pltpu.roll(kt_ref[h], shift, axis=1)[:, 0:1]
            qcol = pltpu.roll(qt_ref[h], shift, axis=1)[:, 0:1]
            aux = aux_ref[bl]
            beta = aux[:, h:h + 1]
            eg = jnp.exp(aux[:, DN_HEADS + h:DN_HEADS + h + 1])
            qk = aux[:, 2 * DN_HEADS + h:2 * DN_HEADS + h + 1]
            s = s_ref[bl, h]
            ks = jnp.sum(kcol * s, axis=0, keepdims=True)
            qs = jnp.sum(qcol * s, axis=0, keepdims=True)
            v_new = beta * (v_ref[bl, :, hs] - eg * ks)
            o = eg * qs + qk * v_new
            snew_ref[bl, h] = s * eg + kcol * v_new
            d_ref[bl, :, hs] = (_rms(o, dn_ref[...]) * _silu(z_ref[bl, :, hs])).astype(d_ref.dtype)
        return carry

    lax.fori_loop(0, nb, body, 0)


def _gdn_sample(s, kt, qt, v, z, aux, dn):
    batch = s.shape[0]
    nb = SB_GDN
    sblk = pl.BlockSpec((nb, DN_HEADS, DN_DK, DN_DK), lambda i: (i, 0, 0, 0))
    rowblk = lambda cols: pl.BlockSpec((nb, 1, cols), lambda i: (i, 0, 0))
    as_rows = lambda a: a.reshape(batch, 1, a.shape[-1])
    snew, d = pl.pallas_call(
        _gdn_sample_kernel,
        grid=(batch // nb,),
        in_specs=[sblk, _const_spec(kt.shape), _const_spec(qt.shape), rowblk(DN_V), rowblk(DN_V),
                  rowblk(LANES), _const_spec((1, DN_DK))],
        out_specs=[sblk, rowblk(DN_V)],
        out_shape=[jax.ShapeDtypeStruct(s.shape, F32), jax.ShapeDtypeStruct((batch, 1, DN_V), F32)],
        compiler_params=pltpu.CompilerParams(dimension_semantics=("arbitrary",),
                                             vmem_limit_bytes=VMEM_LIMIT),
        name="gdn_sample",
    )(s, kt, qt, as_rows(v), as_rows(z), as_rows(aux), dn)
    return snew, d.reshape(batch, DN_V)


def _pad_lanes(vec, offset):
    return jnp.zeros((1, LANES), F32).at[0, offset:offset + vec.shape[0]].set(vec.astype(F32))


def kernel(x_prompt, x_sample, mem_prompt, cache_conv, state_short_conv, state_delta, cache_mem_k, cache_mem_v,
           norm_mix, w_in, conv_w, conv_b, conv_ln_g, conv_ln_b, sc_w, a_log, dt_bias, dn_norm, w_out,
           norm_mem_q, norm_mem_kv, w_mq, w_mk, w_mv, w_mo, norm_ffn, w_gate, w_up, w_down, norm_f):
    depth = w_in.shape[0]
    assert depth == 1, "single-layer stack"
    bp, seq, _ = x_prompt.shape
    bs = x_sample.shape[0]
    l = 0

    row = lambda a: a.reshape(1, -1).astype(F32)
    wmain = w_in[l][:, :MAIN_COLS].astype(BF16)
    wbd = jnp.pad(w_in[l][:, MAIN_COLS:], ((0, 0), (0, LANES - 2 * DN_HEADS))).astype(BF16)
    alog = _pad_lanes(a_log[l], DN_HEADS)
    dtb = _pad_lanes(dt_bias[l], DN_HEADS)
    mixer_w = (row(norm_mix[l]), wmain, wbd, conv_w[l], row(conv_b[l]), row(conv_ln_g[l]), row(conv_ln_b[l]),
               sc_w[l], alog, dtb)
    wout = w_out[l].astype(BF16)
    wmq = w_mq[l].astype(BF16)
    wmo = w_mo[l].astype(BF16)
    ffn_w = (wmo, row(norm_ffn[l]), w_gate[l].astype(BF16), w_up[l].astype(BF16), w_down[l].astype(BF16),
             row(norm_f))
    dn = row(dn_norm[l])

    xp = x_prompt.reshape(bp * seq, D_MODEL)
    mk, mv = _mem_kv(mem_prompt.reshape(bp * N_MEM, D_MODEL), row(norm_mem_kv[l]),
                     w_mk[l].astype(BF16), w_mv[l].astype(BF16))
    c_p, q_p, k_p, v_p, z_p, bg_p, nconv_p, nsc_p = _inproj_prompt(xp, bp, seq, *mixer_w)
    d_p, ndelta_p = _gdn_prompt(q_p, k_p, v_p, z_p, bg_p, dn, bp, seq)
    x1_p, qm_p = _outproj(xp, c_p, d_p, wout, row(norm_mem_q[l]), wmq, TM_O, BF16)
    o_p = _attn_prompt(qm_p, mk, mv, bp, seq)
    y_p = _ffn(x1_p, o_p, *ffn_w, TM_F)

    xs = x_sample.reshape(bs, D_MODEL)
    cache_t = jnp.swapaxes(cache_conv[l], 0, 1)
    state_t = jnp.swapaxes(state_short_conv[l], 0, 1)
    c_s, u_s, pq_s, kt_s, qt_s, v_s, z_s, aux_s = _inproj_sample(xs, *mixer_w, cache_t, state_t)
    ndelta_s, d_s = _gdn_sample(state_delta[l], kt_s, qt_s, v_s, z_s, aux_s, dn)
    x1_s, qm_s = _outproj(xs, c_s, d_s, wout, row(norm_mem_q[l]), wmq, bs, F32)
    o_s = _attn_sample(qm_s.reshape(bs, MEM_HEADS, MEM_HD), cache_mem_k[l], cache_mem_v[l])
    y_s = _ffn(x1_s, o_s.reshape(bs, D_MODEL), *ffn_w, bs)
    nconv_s = jnp.concatenate([cache_conv[l][:, 1:], u_s[:, None, :]], axis=1)
    nsc_s = jnp.concatenate([state_short_conv[l][:, 1:], pq_s[:, None, :]], axis=1)

    return (y_p.reshape(bp, seq, D_MODEL), y_s.reshape(bs, 1, D_MODEL),
            nconv_p[None], nsc_p[None], ndelta_p[None],
            mk.reshape(1, bp, N_MEM, MEM_HEADS, MEM_HD), mv.reshape(1, bp, N_MEM, MEM_HEADS, MEM_HD),
            nconv_s[None], nsc_s[None], ndelta_s[None])
```
